```python
import jax, jax.numpy as jnp
from jax import lax
import numpy as np

D_MODEL = 2048
BATCH = 2
SEQ = 4096
DEPTH = 1
DEC_BATCH = 128
DEC_SEQ = 1
PAST_LEN = 16384
PAGE_SIZE = 128

N_META = 16
Q_BLOCK = 128
SB_HEADS = 8
SB_KV_HEADS = 2
SB_HEAD_DIM = 128
SB_GROUP = SB_HEADS // SB_KV_HEADS
SB_WIDTH = SB_HEADS * SB_HEAD_DIM
SB_SCALE = SB_HEAD_DIM ** -0.5
MLA_HEADS = 8
KV_LORA = 512
NOPE_DIM = 128
ROPE_DIM = 64
MLA_V_DIM = 128
MLA_WIDTH = MLA_HEADS * MLA_V_DIM
MLA_SCALE = (NOPE_DIM + ROPE_DIM) ** -0.5
ROPE_THETA = 10000.0
PEER_HEADS = 8
N_KEYS = 128
N_EXPERTS = N_KEYS * N_KEYS
PEER_QDIM = 256
PEER_HALF = PEER_QDIM // 2
PEER_TOPK = 16
PEER_BLOCK = 128
RMS_EPS = 1e-6
IN_SIZES = (SB_WIDTH, SB_KV_HEADS * SB_HEAD_DIM, SB_KV_HEADS * SB_HEAD_DIM, MLA_HEADS * (NOPE_DIM + ROPE_DIM), KV_LORA, ROPE_DIM, D_MODEL, D_MODEL)
IN_COLS = sum(IN_SIZES)
F32 = jnp.float32

kernel_name = 'stickbreak_mla_peer_hybrid_step'


def rmsnorm(x, g):
    xf = x.astype(F32)
    y = xf * lax.rsqrt(jnp.mean(xf * xf, axis=-1, keepdims=True) + RMS_EPS)
    return (y * g.astype(F32)).astype(x.dtype)


def rope_angles(pos):
    half = ROPE_DIM // 2
    inv = jnp.power(jnp.float32(ROPE_THETA), -jnp.arange(half, dtype=F32) / half)
    ang = pos.astype(F32)[:, None] * inv[None, :]
    return jnp.cos(ang), jnp.sin(ang)


def apply_rope(x, cos, sin):
    xf = x.astype(F32)
    x1, x2 = jnp.split(xf, 2, axis=-1)
    return jnp.concatenate([x1 * cos - x2 * sin, x2 * cos + x1 * sin], axis=-1).astype(x.dtype)


def project(h, pos, w_in, q_nope_g, q_rope_g, ckv_g, krope_g):
    lead = h.shape[:-1]
    offs = np.cumsum(IN_SIZES)[:-1].tolist()
    q_sb, k_sb, v_sb, q_mla, c_raw, kr_raw, g_a, g_b = jnp.split(h @ w_in, offs, axis=-1)
    q_sb = q_sb.reshape(*lead, SB_HEADS, SB_HEAD_DIM)
    k_sb = k_sb.reshape(*lead, SB_KV_HEADS, SB_HEAD_DIM)
    v_sb = v_sb.reshape(*lead, SB_KV_HEADS, SB_HEAD_DIM)
    q_mla = q_mla.reshape(*lead, MLA_HEADS, NOPE_DIM + ROPE_DIM)
    cos, sin = rope_angles(pos)
    q_nope = rmsnorm(q_mla[..., :NOPE_DIM], q_nope_g)
    q_rope = apply_rope(rmsnorm(q_mla[..., NOPE_DIM:], q_rope_g), cos[:, None, :], sin[:, None, :])
    c = rmsnorm(c_raw, ckv_g)
    kr = apply_rope(rmsnorm(kr_raw, krope_g), cos, sin)
    return q_sb, k_sb, v_sb, q_nope, q_rope, c, kr, jax.nn.sigmoid(g_a), jax.nn.sigmoid(g_b)


def stick_breaking_weights(z, mask, log_surv):
    log_keep = jax.nn.log_sigmoid(-z)
    if mask is not None:
        log_keep = jnp.where(mask, log_keep, 0.0)
    later = lax.cumsum(log_keep, axis=z.ndim - 1, reverse=True) - log_keep
    a = jnp.exp(jax.nn.log_sigmoid(z) + later + log_surv[..., None])
    if mask is not None:
        a = jnp.where(mask, a, 0.0)
    return a, log_keep


def mla_scores(q_lat, q_rope, c, kr):
    s = jnp.einsum('bqhc,bkc->bhqk', q_lat, c) + jnp.einsum('bqhr,bkr->bhqk', q_rope, kr)
    return s.astype(F32) * MLA_SCALE


def prompt_attention(q_sb, k_sb, v_sb, q_nope, q_rope, c, kr, w_uk, w_uv):
    B, T = q_sb.shape[:2]
    k_nope = jnp.einsum('btc,chd->bthd', c, w_uk)
    v_mla = jnp.einsum('btc,chd->bthd', c, w_uv)
    starts = [0] + list(range(N_META, T, Q_BLOCK))
    ends = starts[1:] + [T]
    outs_sb, outs_mla = [], []
    for s0, e0 in zip(starts, ends):
        nq = e0 - s0
        qpos = jnp.arange(s0, e0)
        kpos = jnp.arange(e0)
        qg = q_sb[:, s0:e0].reshape(B, nq, SB_KV_HEADS, SB_GROUP, SB_HEAD_DIM)
        z = jnp.einsum('bqgrd,bkgd->bgrqk', qg, k_sb[:, :e0]).astype(F32) * SB_SCALE
        strict = kpos[None, :] < qpos[:, None]
        a, _ = stick_breaking_weights(z, strict, jnp.zeros(z.shape[:-1], F32))
        o = jnp.einsum('bgrqk,bkgd->bqgrd', a, v_sb[:, :e0].astype(F32))
        outs_sb.append(o.reshape(B, nq, SB_WIDTH))
        s = (jnp.einsum('bqhd,bkhd->bhqk', q_nope[:, s0:e0], k_nope[:, :e0])
             + jnp.einsum('bqhr,bkr->bhqk', q_rope[:, s0:e0], kr[:, :e0])).astype(F32) * MLA_SCALE
        s = jnp.where(kpos[None, :] <= qpos[:, None], s, -jnp.inf)
        p = jax.nn.softmax(s, axis=-1)
        o = jnp.einsum('bhqk,bkhd->bqhd', p, v_mla[:, :e0].astype(F32))
        outs_mla.append(o.reshape(B, nq, MLA_WIDTH))
    dt = q_sb.dtype
    return jnp.concatenate(outs_sb, axis=1).astype(dt), jnp.concatenate(outs_mla, axis=1).astype(dt)


def decode_attention(q_sb, k_new, v_new, q_nope, q_rope, c_new, kr_new, w_uk, w_uv, cache_sb_kv, cache_mla, layer, page_table):
    DB, DS = q_sb.shape[:2]
    qg = q_sb.reshape(DB, DS, SB_KV_HEADS, SB_GROUP, SB_HEAD_DIM)
    q_lat = jnp.einsum('bqhd,chd->bqhc', q_nope, w_uk)
    pos = jnp.arange(DS)
    strict = pos[None, :] < pos[:, None]
    causal = pos[None, :] <= pos[:, None]
    z = jnp.einsum('bqgrd,bkgd->bgrqk', qg, k_new).astype(F32) * SB_SCALE
    a, log_keep = stick_breaking_weights(z, strict, jnp.zeros(z.shape[:-1], F32))
    sb_acc = jnp.einsum('bgrqk,bkgd->bgrqd', a, v_new.astype(F32))
    sb_surv = log_keep.sum(-1)
    s = jnp.where(causal, mla_scores(q_lat, q_rope, c_new, kr_new), -jnp.inf)
    m = s.max(-1)
    p = jnp.exp(s - m[..., None])
    l = p.sum(-1)
    acc = jnp.einsum('bhqk,bkc->bhqc', p, c_new.astype(F32))

    def page_step(carry, phys):
        sb_acc, sb_surv, m, l, acc = carry
        kv = cache_sb_kv[layer, phys]
        lat = cache_mla[layer, phys]
        z = jnp.einsum('bqgrd,bkgd->bgrqk', qg, kv[:, :, 0]).astype(F32) * SB_SCALE
        a, log_keep = stick_breaking_weights(z, None, sb_surv)
        sb_acc = sb_acc + jnp.einsum('bgrqk,bkgd->bgrqd', a, kv[:, :, 1].astype(F32))
        sb_surv = sb_surv + log_keep.sum(-1)
        c_pg = lat[..., :KV_LORA]
        s = mla_scores(q_lat, q_rope, c_pg, lat[..., KV_LORA:])
        m_new = jnp.maximum(m, s.max(-1))
        alpha = jnp.exp(m - m_new)
        p = jnp.exp(s - m_new[..., None])
        l = l * alpha + p.sum(-1)
        acc = acc * alpha[..., None] + jnp.einsum('bhqk,bkc->bhqc', p, c_pg.astype(F32))
        return (sb_acc, sb_surv, m_new, l, acc), None

    (sb_acc, _, _, l, acc), _ = lax.scan(page_step, (sb_acc, sb_surv, m, l, acc), page_table.T, reverse=True)
    o_sb = sb_acc.transpose(0, 3, 1, 2, 4).reshape(DB, DS, SB_WIDTH)
    o_mla = jnp.einsum('bhqc,chd->bqhd', acc / l[..., None], w_uv.astype(F32)).reshape(DB, DS, MLA_WIDTH)
    dt = q_sb.dtype
    return o_sb.astype(dt), o_mla.astype(dt)


def merge(x, o_sb, o_mla, gate_a, gate_b, w_pa, w_pb, bnorm_a_g, bnorm_b_g, w_o):
    ya = rmsnorm(o_sb @ w_pa, bnorm_a_g)
    yb = rmsnorm(o_mla @ w_pb, bnorm_b_g)
    return x + (gate_a * ya + gate_b * yb) @ w_o


def peer(h, w_pq, sub_keys, expert_u, expert_v):
    shape = h.shape
    xt = h.reshape(-1, D_MODEL)
    n = xt.shape[0]
    xt = jnp.pad(xt, ((0, (-n) % PEER_BLOCK), (0, 0)))
    blocks = xt.reshape(-1, PEER_BLOCK, D_MODEL)

    def block_fn(xb):
        q = (xb @ w_pq).reshape(PEER_BLOCK, PEER_HEADS, 2, PEER_HALF)
        s = jnp.einsum('nhpd,hpkd->nhpk', q, sub_keys).astype(F32)
        s_top, i_top = lax.top_k(s, PEER_TOPK)
        cand_s = (s_top[:, :, 0, :, None] + s_top[:, :, 1, None, :]).reshape(PEER_BLOCK, PEER_HEADS, PEER_TOPK * PEER_TOPK)
        cand_i = (i_top[:, :, 0, :, None] * N_KEYS + i_top[:, :, 1, None, :]).reshape(PEER_BLOCK, PEER_HEADS, PEER_TOPK * PEER_TOPK)
        best_s, best_pos = lax.top_k(cand_s, PEER_TOPK)
        idx = jnp.take_along_axis(cand_i, best_pos, axis=-1)
        gate = jax.nn.softmax(best_s, axis=-1)
        act = jax.nn.gelu(jnp.einsum('nhkd,nd->nhk', expert_u[idx], xb).astype(F32), approximate=False)
        return jnp.einsum('nhk,nhkd->nd', (gate * act).astype(xb.dtype), expert_v[idx])

    out = lax.map(block_fn, blocks)
    return out.reshape(-1, D_MODEL)[:n].reshape(shape)


def setup_inputs(seed: int = 0) -> dict:
    key = jax.random.key(seed)
    ks = jax.random.split(key, 24)
    n_pages = PAST_LEN // PAGE_SIZE
    n_used = DEC_BATCH * n_pages
    n_pool = n_used + max(1, n_used // 4)

    def nrm(k, shape, scale=1.0):
        return jax.random.normal(k, shape, F32) * scale

    def gain(k, shape):
        return 1.0 + 0.02 * jax.random.normal(k, shape, F32)

    page_table = jax.random.permutation(ks[4], n_pool)[:n_used].reshape(DEC_BATCH, n_pages).astype(jnp.int32)
    return {
        'x_prompt': nrm(ks[0], (BATCH, SEQ, D_MODEL)),
        'x_sample': nrm(ks[1], (DEC_BATCH, DEC_SEQ, D_MODEL)),
        'cache_sb_kv': nrm(ks[2], (DEPTH, n_pool, PAGE_SIZE, 2, SB_KV_HEADS, SB_HEAD_DIM)),
        'cache_mla': nrm(ks[3], (DEPTH, n_pool, PAGE_SIZE, KV_LORA + ROPE_DIM)),
        'page_table': page_table,
        'meta_tokens': nrm(ks[5], (N_META, D_MODEL)),
        'norm1_g': gain(ks[6], (DEPTH, D_MODEL)),
        'w_in': nrm(ks[7], (DEPTH, D_MODEL, IN_COLS), D_MODEL ** -0.5),
        'q_nope_g': gain(ks[8], (DEPTH, NOPE_DIM)),
        'q_rope_g': gain(ks[9], (DEPTH, ROPE_DIM)),
        'ckv_g': gain(ks[10], (DEPTH, KV_LORA)),
        'krope_g': gain(ks[11], (DEPTH, ROPE_DIM)),
        'w_uk': nrm(ks[12], (DEPTH, KV_LORA, MLA_HEADS, NOPE_DIM), KV_LORA ** -0.5),
        'w_uv': nrm(ks[13], (DEPTH, KV_LORA, MLA_HEADS, MLA_V_DIM), KV_LORA ** -0.5),
        'w_pa': nrm(ks[14], (DEPTH, SB_WIDTH, D_MODEL), SB_WIDTH ** -0.5),
        'w_pb': nrm(ks[15], (DEPTH, MLA_WIDTH, D_MODEL), MLA_WIDTH ** -0.5),
        'bnorm_a_g': gain(ks[16], (DEPTH, D_MODEL)),
        'bnorm_b_g': gain(ks[17], (DEPTH, D_MODEL)),
        'w_o': nrm(ks[18], (DEPTH, D_MODEL, D_MODEL), D_MODEL ** -0.5),
        'norm2_g': gain(ks[19], (DEPTH, D_MODEL)),
        'w_pq': nrm(ks[20], (DEPTH, D_MODEL, PEER_HEADS * PEER_QDIM), D_MODEL ** -0.5),
        'sub_keys': nrm(ks[21], (DEPTH, PEER_HEADS, 2, N_KEYS, PEER_HALF), PEER_HALF ** -0.5),
        'expert_u': nrm(ks[22], (DEPTH, N_EXPERTS, D_MODEL), D_MODEL ** -0.5),
        'expert_v': nrm(ks[23], (DEPTH, N_EXPERTS, D_MODEL), 0.25),
    }


def reference(x_prompt, x_sample, cache_sb_kv, cache_mla, page_table, meta_tokens, norm1_g, w_in, q_nope_g, q_rope_g, ckv_g, krope_g, w_uk, w_uv, w_pa, w_pb, bnorm_a_g, bnorm_b_g, w_o, norm2_g, w_pq, sub_keys, expert_u, expert_v):
    B = x_prompt.shape[0]
    meta = jnp.broadcast_to(meta_tokens[None].astype(x_prompt.dtype), (B, N_META, D_MODEL))
    xp = jnp.concatenate([meta, x_prompt], axis=1)
    xs = x_sample
    pos_p = jnp.arange(xp.shape[1])
    pos_s = PAST_LEN + jnp.arange(xs.shape[1])
    sb_p, mla_p, sb_s, mla_s = [], [], [], []
    for l in range(DEPTH):
        q_sb, k_sb, v_sb, q_nope, q_rope, c, kr, g_a, g_b = project(rmsnorm(xp, norm1_g[l]), pos_p, w_in[l], q_nope_g[l], q_rope_g[l], ckv_g[l], krope_g[l])
        o_sb, o_mla = prompt_attention(q_sb, k_sb, v_sb, q_nope, q_rope, c, kr, w_uk[l], w_uv[l])
        xp = merge(xp, o_sb, o_mla, g_a, g_b, w_pa[l], w_pb[l], bnorm_a_g[l], bnorm_b_g[l], w_o[l])
        xp = xp + peer(rmsnorm(xp, norm2_g[l]), w_pq[l], sub_keys[l], expert_u[l], expert_v[l])
        sb_p.append(jnp.stack([k_sb, v_sb], axis=2))
        mla_p.append(jnp.concatenate([c, kr], axis=-1))
        q_sb, k_sb, v_sb, q_nope, q_rope, c, kr, g_a, g_b = project(rmsnorm(xs, norm1_g[l]), pos_s, w_in[l], q_nope_g[l], q_rope_g[l], ckv_g[l], krope_g[l])
        o_sb, o_mla = decode_attention(q_sb, k_sb, v_sb, q_nope, q_rope, c, kr, w_uk[l], w_uv[l], cache_sb_kv, cache_mla, l, page_table)
        xs = merge(xs, o_sb, o_mla, g_a, g_b, w_pa[l], w_pb[l], bnorm_a_g[l], bnorm_b_g[l], w_o[l])
        xs = xs + peer(rmsnorm(xs, norm2_g[l]), w_pq[l], sub_keys[l], expert_u[l], expert_v[l])
        sb_s.append(jnp.stack([k_sb, v_sb], axis=2))
        mla_s.append(jnp.concatenate([c, kr], axis=-1))
    return (xp[:, N_META:], xs, jnp.stack(sb_p), jnp.stack(mla_p), jnp.stack(sb_s), jnp.stack(mla_s))
```

```python
import functools
import math

import jax
import jax.numpy as jnp
import numpy as np
from jax import lax
from jax.experimental import pallas as pl
from jax.experimental.pallas import tpu as pltpu

F32 = jnp.float32
BF16 = jnp.bfloat16

D_MODEL = 2048
N_META = 16
SB_HEADS = 8
SB_KV_HEADS = 2
SB_GROUP = SB_HEADS // SB_KV_HEADS
HEAD_DIM = 128
SB_WIDTH = SB_HEADS * HEAD_DIM
SB_SCALE = HEAD_DIM ** -0.5
MLA_HEADS = 8
KV_LORA = 512
NOPE_DIM = 128
ROPE_DIM = 64
MLA_V_DIM = 128
MLA_WIDTH = MLA_HEADS * MLA_V_DIM
MLA_SCALE = (NOPE_DIM + ROPE_DIM) ** -0.5
MLA_CACHE_W = KV_LORA + ROPE_DIM
ROPE_THETA = 10000.0
PEER_HEADS = 8
N_KEYS = 128
PEER_HALF = 128
PEER_TOPK = 16
RMS_EPS = 1e-6

LANE = 128
ROW_TILE = 512
SEQ_TILE = 128
VMEM_LIMIT = 56 * 1024 * 1024

C_QSB = 0
C_QNOPE = 1024
C_GA = 2048
C_GB = 4096
C_KSB = 6144
C_VSB = 6400
C_QROPE = 6656
C_CKV = 7168
C_KR = 7680
PROJ_COLS = 7808

QCAT_W = 2 * LANE


def _cparams(sem, vmem=VMEM_LIMIT):
    return pltpu.CompilerParams(dimension_semantics=sem, vmem_limit_bytes=vmem)


def _rmsnorm_kernel(x_ref, g_ref, o_ref):
    x = x_ref[...]
    r = lax.rsqrt(jnp.mean(x * x, axis=-1, keepdims=True) + RMS_EPS)
    o_ref[...] = (x * r * g_ref[...]).astype(o_ref.dtype)


def _rmsnorm(x, g, tm=256):
    m, d = x.shape
    return pl.pallas_call(
        _rmsnorm_kernel,
        grid=(m // tm,),
        in_specs=[pl.BlockSpec((tm, d), lambda i: (i, 0)),
                  pl.BlockSpec((1, d), lambda i: (0, 0))],
        out_specs=pl.BlockSpec((tm, d), lambda i: (i, 0)),
        out_shape=jax.ShapeDtypeStruct((m, d), BF16),
        compiler_params=_cparams(("parallel",)),
        name="rmsnorm",
    )(x, g.reshape(1, d))


def _mm_kernel(x_ref, w_ref, o_ref):
    o_ref[...] = jnp.dot(x_ref[...], w_ref[...], preferred_element_type=F32).astype(o_ref.dtype)


def _matmul(x, w, tm, tn, out_dtype, name):
    m, k = x.shape
    n = w.shape[1]
    return pl.pallas_call(
        _mm_kernel,
        grid=(pl.cdiv(n, tn), m // tm),
        in_specs=[pl.BlockSpec((tm, k), lambda j, i: (i, 0)),
                  pl.BlockSpec((k, tn), lambda j, i: (0, j))],
        out_specs=pl.BlockSpec((tm, tn), lambda j, i: (i, j)),
        out_shape=jax.ShapeDtypeStruct((m, n), out_dtype),
        compiler_params=_cparams(("parallel", "parallel")),
        name=name,
    )(x, w)


def _swap_halves_32(x):
    lane = lax.broadcasted_iota(jnp.int32, x.shape, 1)
    first = (lane % ROPE_DIM) < (ROPE_DIM // 2)
    return jnp.where(first, pltpu.roll(x, LANE - ROPE_DIM // 2, 1), pltpu.roll(x, ROPE_DIM // 2, 1))


def _postproj_kernel(qn_ref, qr_ref, c_ref, kr_ref, cos_ref, sin_ref, gqn_ref, gqr_ref, gc_ref, gkr_ref,
                     qcat_ref, mla_ref):
    cos = cos_ref[...]
    sin = sin_ref[...]
    lane = lax.broadcasted_iota(jnp.int32, cos.shape, 1)
    lo = lane < ROPE_DIM

    def norm64(x, g):
        sq = x * x
        s_lo = jnp.sum(jnp.where(lo, sq, 0.0), axis=-1, keepdims=True)
        s_hi = jnp.sum(jnp.where(lo, 0.0, sq), axis=-1, keepdims=True)
        ms = jnp.where(lo, s_lo, s_hi) * (1.0 / ROPE_DIM)
        return x * lax.rsqrt(ms + RMS_EPS) * g

    def rope(x):
        return x * cos + _swap_halves_32(x) * sin

    zeros = jnp.zeros_like(cos)
    for h in range(MLA_HEADS):
        xn = qn_ref[:, h * NOPE_DIM:(h + 1) * NOPE_DIM]
        xn = xn * lax.rsqrt(jnp.mean(xn * xn, axis=-1, keepdims=True) + RMS_EPS) * gqn_ref[...]
        qcat_ref[:, h * QCAT_W:h * QCAT_W + NOPE_DIM] = xn.astype(qcat_ref.dtype)
    for j in range(MLA_HEADS // 2):
        xr = rope(norm64(qr_ref[:, j * LANE:(j + 1) * LANE], gqr_ref[...]))
        even = jnp.where(lo, xr, zeros)
        odd = jnp.where(lo, pltpu.roll(xr, ROPE_DIM, 1), zeros)
        for h, val in ((2 * j, even), (2 * j + 1, odd)):
            qcat_ref[:, h * QCAT_W + NOPE_DIM:(h + 1) * QCAT_W] = val.astype(qcat_ref.dtype)

    c = c_ref[...]
    c = c * lax.rsqrt(jnp.mean(c * c, axis=-1, keepdims=True) + RMS_EPS) * gc_ref[...]
    mla_ref[:, :KV_LORA] = c
    kr = jnp.where(lo, kr_ref[...], zeros)
    kr = rope(norm64(kr, gkr_ref[...]))
    mla_ref[:, KV_LORA:] = kr[:, :ROPE_DIM]


def _postproj(proj, cos_t, sin_t, q_nope_g, q_rope_g, ckv_g, krope_g, tm=256):
    m = proj.shape[0]
    gqr = jnp.concatenate([q_rope_g, q_rope_g]).reshape(1, LANE)
    gkr = jnp.concatenate([krope_g, jnp.zeros_like(krope_g)]).reshape(1, LANE)
    row = lambda w, blk: pl.BlockSpec((tm, w), lambda i, blk=blk: (i, blk))
    const = lambda w: pl.BlockSpec((1, w), lambda i: (0, 0))
    return pl.pallas_call(
        _postproj_kernel,
        grid=(m // tm,),
        in_specs=[row(1024, C_QNOPE // 1024), row(512, C_QROPE // 512), row(512, C_CKV // 512),
                  row(LANE, C_KR // LANE), row(LANE, 0), row(LANE, 0),
                  const(NOPE_DIM), const(LANE), const(KV_LORA), const(LANE)],
        out_specs=[pl.BlockSpec((tm, MLA_HEADS * QCAT_W), lambda i: (i, 0)),
                   pl.BlockSpec((tm, MLA_CACHE_W), lambda i: (i, 0))],
        out_shape=[jax.ShapeDtypeStruct((m, MLA_HEADS * QCAT_W), BF16),
                   jax.ShapeDtypeStruct((m, MLA_CACHE_W), F32)],
        compiler_params=_cparams(("parallel",)),
        name="postproj",
    )(proj, proj, proj, proj, cos_t, sin_t, q_nope_g.reshape(1, -1), gqr, ckv_g.reshape(1, -1), gkr)


def _kvup_kernel(mla_ref, w_ref, kcat_ref, v_ref):
    mla = mla_ref[...]
    c = mla[:, :KV_LORA].astype(BF16)
    kv = jnp.dot(c, w_ref[...], preferred_element_type=F32)
    kr = mla[:, KV_LORA:]
    kr2 = jnp.concatenate([kr, jnp.zeros_like(kr)], axis=-1).astype(kcat_ref.dtype)
    for h in range(MLA_HEADS):
        kcat_ref[:, h * QCAT_W:h * QCAT_W + NOPE_DIM] = kv[:, h * NOPE_DIM:(h + 1) * NOPE_DIM].astype(kcat_ref.dtype)
        kcat_ref[:, h * QCAT_W + NOPE_DIM:(h + 1) * QCAT_W] = kr2
    v_ref[...] = kv[:, MLA_HEADS * NOPE_DIM:].astype(v_ref.dtype)


def _kvup(mla, w_ukv, rows, tm=256):
    return pl.pallas_call(
        _kvup_kernel,
        grid=(rows // tm,),
        in_specs=[pl.BlockSpec((tm, MLA_CACHE_W), lambda i: (i, 0)),
                  pl.BlockSpec(w_ukv.shape, lambda i: (0, 0))],
        out_specs=[pl.BlockSpec((tm, MLA_HEADS * QCAT_W), lambda i: (i, 0)),
                   pl.BlockSpec((tm, MLA_WIDTH), lambda i: (i, 0))],
        out_shape=[jax.ShapeDtypeStruct((rows, MLA_HEADS * QCAT_W), BF16),
                   jax.ShapeDtypeStruct((rows, MLA_WIDTH), BF16)],
        compiler_params=_cparams(("parallel",)),
        name="kvup",
    )(mla, w_ukv)


def _softplus(z):
    return jnp.maximum(z, 0.0) + jnp.log1p(jnp.exp(-jnp.abs(z)))


def _later_sum(lk, tri):
    hi = lk.astype(BF16)
    lo = (lk - hi.astype(F32)).astype(BF16)
    n = lk.shape[0]
    both = jnp.dot(jnp.concatenate([hi, lo], axis=0), tri, preferred_element_type=F32)
    return both[:n] + both[n:]


def _tri_matrix(n):
    j = np.arange(n)
    return jnp.asarray((j[:, None] > j[None, :]).astype(np.float32), dtype=BF16)


def _sb_prompt_kernel(q_ref, k_ref, v_ref, tri_ref, o_ref):
    i = pl.program_id(2)
    tq = SEQ_TILE
    q4 = q_ref[...]
    qs = jnp.concatenate([q4[:, r * HEAD_DIM:(r + 1) * HEAD_DIM] for r in range(SB_GROUP)], axis=0).astype(BF16)
    tri = tri_ref[...]
    rows = SB_GROUP * tq

    def block(j, masked):
        start = pl.multiple_of(j * tq, tq)
        kb = k_ref[pl.ds(start, tq), :].astype(BF16)
        vb = v_ref[pl.ds(start, tq), :].astype(BF16)
        z = lax.dot_general(qs, kb, (((1,), (1,)), ((), ())), preferred_element_type=F32) * SB_SCALE
        sp = _softplus(z)
        lk = -sp
        if masked:
            qpos = lax.broadcasted_iota(jnp.int32, (rows, tq), 0) % tq
            kpos = lax.broadcasted_iota(jnp.int32, (rows, tq), 1)
            valid = kpos < qpos
            lk = jnp.where(valid, lk, 0.0)
        return z - sp, lk, vb, (valid if masked else None)

    def weights(ls, lk, surv, valid):
        a = jnp.exp(ls + _later_sum(lk, tri) + surv)
        if valid is not None:
            a = jnp.where(valid, a, 0.0)
        return a

    ls, lk, vb, valid = block(i, True)
    a = weights(ls, lk, jnp.zeros((rows, 1), F32), valid)
    acc = jnp.dot(a.astype(BF16), vb, preferred_element_type=F32)
    surv = jnp.sum(lk, axis=-1, keepdims=True)

    def body(t, carry):
        acc, surv = carry
        ls, lk, vb, _ = block(i - 1 - t, False)
        a = weights(ls, lk, surv, None)
        acc = acc + jnp.dot(a.astype(BF16), vb, preferred_element_type=F32)
        return acc, surv + jnp.sum(lk, axis=-1, keepdims=True)

    acc, _ = lax.fori_loop(0, i, body, (acc, surv))
    for r in range(SB_GROUP):
        o_ref[:, r * HEAD_DIM:(r + 1) * HEAD_DIM] = acc[r * tq:(r + 1) * tq].astype(o_ref.dtype)


def _sb_prompt(proj, batch, tp):
    nq = tp // SEQ_TILE
    gw = SB_GROUP * HEAD_DIM
    return pl.pallas_call(
        _sb_prompt_kernel,
        grid=(batch, SB_KV_HEADS, nq),
        in_specs=[pl.BlockSpec((SEQ_TILE, gw), lambda b, g, i: (b * nq + i, C_QSB // gw + g)),
                  pl.BlockSpec((tp, HEAD_DIM), lambda b, g, i: (b, C_KSB // HEAD_DIM + g)),
                  pl.BlockSpec((tp, HEAD_DIM), lambda b, g, i: (b, C_VSB // HEAD_DIM + g)),
                  pl.BlockSpec((SEQ_TILE, SEQ_TILE), lambda b, g, i: (0, 0))],
        out_specs=pl.BlockSpec((SEQ_TILE, gw), lambda b, g, i: (b * nq + i, g)),
        out_shape=jax.ShapeDtypeStruct((batch * tp, SB_WIDTH), BF16),
        compiler_params=_cparams(("parallel", "parallel", "arbitrary")),
        name="sb_prompt",
    )(proj, proj, proj, _tri_matrix(SEQ_TILE))


def _mla_prompt_kernel(q_ref, k_ref, v_ref, o_ref):
    i = pl.program_id(2)
    tq = SEQ_TILE
    q = q_ref[...]

    def scores(j):
        start = pl.multiple_of(j * tq, tq)
        kb = k_ref[pl.ds(start, tq), :]
        vb = v_ref[pl.ds(start, tq), :]
        s = lax.dot_general(q, kb, (((1,), (1,)), ((), ())), preferred_element_type=F32) * MLA_SCALE
        return s, vb

    s, vb = scores(i)
    qpos = lax.broadcasted_iota(jnp.int32, (tq, tq), 0)
    kpos = lax.broadcasted_iota(jnp.int32, (tq, tq), 1)
    s = jnp.where(kpos <= qpos, s, -jnp.inf)
    m = jnp.max(s, axis=-1, keepdims=True)
    p = jnp.exp(s - m)
    l = jnp.sum(p, axis=-1, keepdims=True)
    acc = jnp.dot(p.astype(BF16), vb, preferred_element_type=F32)

    def body(j, carry):
        m, l, acc = carry
        s, vb = scores(j)
        m_new = jnp.maximum(m, jnp.max(s, axis=-1, keepdims=True))
        alpha = jnp.exp(m - m_new)
        p = jnp.exp(s - m_new)
        l = l * alpha + jnp.sum(p, axis=-1, keepdims=True)
        acc = acc * alpha + jnp.dot(p.astype(BF16), vb, preferred_element_type=F32)
        return m_new, l, acc

    m, l, acc = lax.fori_loop(0, i, body, (m, l, acc))
    o_ref[...] = (acc / l).astype(o_ref.dtype)


def _mla_prompt(qcat, kcat, v, batch, tp):
    nq = tp // SEQ_TILE
    return pl.pallas_call(
        _mla_prompt_kernel,
        grid=(batch, MLA_HEADS, nq),
        in_specs=[pl.BlockSpec((SEQ_TILE, QCAT_W), lambda b, h, i: (b * nq + i, h)),
                  pl.BlockSpec((tp, QCAT_W), lambda b, h, i: (b, h)),
                  pl.BlockSpec((tp, MLA_V_DIM), lambda b, h, i: (b, h))],
        out_specs=pl.BlockSpec((SEQ_TILE, MLA_V_DIM), lambda b, h, i: (b * nq + i, h)),
        out_shape=jax.ShapeDtypeStruct((batch * tp, MLA_WIDTH), BF16),
        compiler_params=_cparams(("parallel", "parallel", "arbitrary")),
        name="mla_prompt",
    )(qcat, kcat, v)


def _qlat_kernel(qcat_ref, wuk_ref, o_ref):
    for h in range(MLA_HEADS):
        qn = qcat_ref[:, h * QCAT_W:h * QCAT_W + NOPE_DIM]
        w = wuk_ref[:, h * NOPE_DIM:(h + 1) * NOPE_DIM]
        ql = lax.dot_general(qn, w, (((1,), (1,)), ((), ())), preferred_element_type=F32)
        o_ref[h, :, :KV_LORA] = ql.astype(o_ref.dtype)
        o_ref[h, :, KV_LORA:] = qcat_ref[:, h * QCAT_W + NOPE_DIM:h * QCAT_W + NOPE_DIM + ROPE_DIM]


def _qlat(qcat, w_uk2, row0, db):
    return pl.pallas_call(
        _qlat_kernel,
        grid=(1,),
        in_specs=[pl.BlockSpec((db, MLA_HEADS * QCAT_W), lambda i: (row0 // db, 0)),
                  pl.BlockSpec(w_uk2.shape, lambda i: (0, 0))],
        out_specs=pl.BlockSpec((MLA_HEADS, db, MLA_CACHE_W), lambda i: (0, 0, 0)),
        out_shape=jax.ShapeDtypeStruct((MLA_HEADS, db, MLA_CACHE_W), BF16),
        compiler_params=_cparams(("arbitrary",)),
        name="decode_qlat",
    )(qcat, w_uk2)


def _decode_kernel(pps, pt_ref, qsb_ref, qml_ref, knew_ref, lnew_ref, *rest):
    sb_refs = rest[:pps]
    ml_refs = rest[pps:2 * pps]
    tri_ref = rest[2 * pps]
    osb_ref, oacc_ref = rest[2 * pps + 1:2 * pps + 3]
    sbacc_ref, surv_ref, m_ref, l_ref, acc_ref = rest[2 * pps + 3:]
    s_id = pl.program_id(1)
    head = lax.broadcasted_iota(jnp.int32, (SB_HEADS, HEAD_DIM), 0)
    grp0 = head < SB_GROUP
    q = qsb_ref[0]
    qml = qml_ref[0]

    @pl.when(s_id == 0)
    def _():
        knew = knew_ref[0]
        k_h = jnp.where(grp0, knew[:, 0:HEAD_DIM], knew[:, HEAD_DIM:2 * HEAD_DIM])
        v_h = jnp.where(grp0, knew[:, 2 * HEAD_DIM:3 * HEAD_DIM], knew[:, 3 * HEAD_DIM:4 * HEAD_DIM])
        z = jnp.sum(q * k_h, axis=-1, keepdims=True) * SB_SCALE
        pos = lax.broadcasted_iota(jnp.int32, z.shape, 1)
        strict = pos < pos
        sp = _softplus(z)
        a = jnp.where(strict, jnp.exp(z - sp), 0.0)
        sbacc_ref[...] = a * v_h
        surv_ref[...] = jnp.broadcast_to(jnp.where(strict, -sp, 0.0), surv_ref.shape)
        lnew = lnew_ref[0]
        s_new = jnp.sum(qml.astype(F32) * lnew, axis=-1, keepdims=True) * MLA_SCALE
        m_ref[...] = jnp.broadcast_to(s_new, m_ref.shape)
        l_ref[...] = jnp.ones_like(l_ref)
        acc_ref[...] = jnp.broadcast_to(lnew[:, :KV_LORA], acc_ref.shape)

    tri = tri_ref[...]
    qb = q.astype(BF16)
    surv = surv_ref[:, 0:1]
    sb_terms = []
    for r in range(pps):
        kv = sb_refs[r][0]
        kboth = jnp.concatenate([kv[:, 0:HEAD_DIM], kv[:, HEAD_DIM:2 * HEAD_DIM]], axis=0).astype(BF16)
        vboth = jnp.concatenate([kv[:, 2 * HEAD_DIM:3 * HEAD_DIM], kv[:, 3 * HEAD_DIM:]], axis=0).astype(BF16)
        zall = lax.dot_general(qb, kboth, (((1,), (1,)), ((), ())), preferred_element_type=F32) * SB_SCALE
        z = jnp.where(grp0, zall[:, :HEAD_DIM], zall[:, HEAD_DIM:])
        sp = _softplus(z)
        lk = -sp
        later = _later_sum(lk, tri)
        sb_terms.append((z - sp + later, jnp.sum(lk, axis=-1, keepdims=True), vboth))
    sb_new = sbacc_ref[...]
    for ls, tot, vboth in sb_terms:
        a = jnp.exp(ls + surv)
        aboth = jnp.concatenate([jnp.where(grp0, a, 0.0), jnp.where(grp0, 0.0, a)], axis=1).astype(BF16)
        sb_new = sb_new + jnp.dot(aboth, vboth, preferred_element_type=F32)
        surv = surv + tot
    sbacc_ref[...] = sb_new
    surv_ref[...] = jnp.broadcast_to(surv, surv_ref.shape)

    lats = [ml_refs[r][0].astype(BF16) for r in range(pps)]
    ss = [lax.dot_general(qml, lat, (((1,), (1,)), ((), ())), preferred_element_type=F32) * MLA_SCALE
          for lat in lats]
    m_old = m_ref[:, 0:1]
    m_new = m_old
    for s in ss:
        m_new = jnp.maximum(m_new, jnp.max(s, axis=-1, keepdims=True))
    alpha = jnp.exp(m_old - m_new)
    l_new = l_ref[:, 0:1] * alpha
    acc = acc_ref[...] * alpha
    for s, lat in zip(ss, lats):
        p = jnp.exp(s - m_new)
        l_new = l_new + jnp.sum(p, axis=-1, keepdims=True)
        acc = acc + jnp.dot(p.astype(BF16), lat[:, :KV_LORA], preferred_element_type=F32)
    m_ref[...] = jnp.broadcast_to(m_new, m_ref.shape)
    l_ref[...] = jnp.broadcast_to(l_new, l_ref.shape)
    acc_ref[...] = acc

    @pl.when(s_id == pl.num_programs(1) - 1)
    def _():
        osb_ref[0] = sb_new.astype(osb_ref.dtype)
        oacc_ref[0] = (acc / l_new).astype(oacc_ref.dtype)


def _pages_per_step(n_pages):
    for pps in (8, 4, 2, 1):
        if n_pages % pps == 0:
            return pps


def _decode(qsb, qml, knew, lnew, cache_sb, cache_mla, page_table):
    db, n_pages = page_table.shape
    pps = _pages_per_step(n_pages)
    page = cache_sb.shape[1]

    def page_spec(width, r):
        return pl.BlockSpec((1, page, width),
                            lambda b, s, pt, r=r: (pt[b, n_pages - 1 - (s * pps + r)], 0, 0))

    per_req = lambda shape: pl.BlockSpec((1,) + shape, lambda b, s, pt: (b, 0, 0))
    grid_spec = pltpu.PrefetchScalarGridSpec(
        num_scalar_prefetch=1,
        grid=(db, n_pages // pps),
        in_specs=([per_req((SB_HEADS, HEAD_DIM)), per_req((MLA_HEADS, MLA_CACHE_W)),
                   per_req((1, 4 * HEAD_DIM)), per_req((1, MLA_CACHE_W))]
                  + [page_spec(4 * HEAD_DIM, r) for r in range(pps)]
                  + [page_spec(MLA_CACHE_W, r) for r in range(pps)]
                  + [pl.BlockSpec((page, page), lambda b, s, pt: (0, 0))]),
        out_specs=[per_req((SB_HEADS, HEAD_DIM)), per_req((MLA_HEADS, KV_LORA))],
        scratch_shapes=[pltpu.VMEM((SB_HEADS, HEAD_DIM), F32), pltpu.VMEM((SB_HEADS, LANE), F32),
                        pltpu.VMEM((MLA_HEADS, LANE), F32), pltpu.VMEM((MLA_HEADS, LANE), F32),
                        pltpu.VMEM((MLA_HEADS, KV_LORA), F32)],
    )
    return pl.pallas_call(
        functools.partial(_decode_kernel, pps),
        grid_spec=grid_spec,
        out_shape=[jax.ShapeDtypeStruct((db, SB_HEADS, HEAD_DIM), BF16),
                   jax.ShapeDtypeStruct((db, MLA_HEADS, KV_LORA), BF16)],
        compiler_params=_cparams(("parallel", "arbitrary")),
        name="decode_attn",
    )(page_table, qsb, qml, knew, lnew, *([cache_sb] * pps), *([cache_mla] * pps), _tri_matrix(page))


def _uvout_kernel(acc_ref, wuv_ref, o_ref):
    for h in range(MLA_HEADS):
        o_ref[:, h * MLA_V_DIM:(h + 1) * MLA_V_DIM] = jnp.dot(
            acc_ref[h], wuv_ref[:, h * MLA_V_DIM:(h + 1) * MLA_V_DIM],
            preferred_element_type=F32).astype(o_ref.dtype)


def _uvout(acc_hm, w_uv2):
    h, db, _ = acc_hm.shape
    return pl.pallas_call(
        _uvout_kernel,
        grid=(1,),
        in_specs=[pl.BlockSpec(acc_hm.shape, lambda i: (0, 0, 0)),
                  pl.BlockSpec(w_uv2.shape, lambda i: (0, 0))],
        out_specs=pl.BlockSpec((db, MLA_WIDTH), lambda i: (0, 0)),
        out_shape=jax.ShapeDtypeStruct((db, MLA_WIDTH), BF16),
        compiler_params=_cparams(("arbitrary",)),
        name="decode_uvout",
    )(acc_hm, w_uv2)


def _merge_kernel(osb_ref, omla_ref, wpa_ref, wpb_ref, ga_ref, gb_ref, gna_ref, gnb_ref, o_ref):
    def branch(o, w, gn, gate):
        y = jnp.dot(o[...], w[...], preferred_element_type=F32)
        y = y * lax.rsqrt(jnp.mean(y * y, axis=-1, keepdims=True) + RMS_EPS) * gn[...]
        return jax.nn.sigmoid(gate[...]) * y

    o_ref[...] = (branch(osb_ref, wpa_ref, gna_ref, ga_ref)
                  + branch(omla_ref, wpb_ref, gnb_ref, gb_ref)).astype(o_ref.dtype)


def _merge(o_sb, o_mla, w_pa, w_pb, proj, bnorm_a_g, bnorm_b_g, tm=256):
    m = o_sb.shape[0]
    row = lambda w, blk: pl.BlockSpec((tm, w), lambda i, blk=blk: (i, blk))
    full = lambda a: pl.BlockSpec(a.shape, lambda i: (0, 0))
    const = pl.BlockSpec((1, D_MODEL), lambda i: (0, 0))
    return pl.pallas_call(
        _merge_kernel,
        grid=(m // tm,),
        in_specs=[row(SB_WIDTH, 0), row(MLA_WIDTH, 0), full(w_pa), full(w_pb),
                  row(D_MODEL, C_GA // D_MODEL), row(D_MODEL, C_GB // D_MODEL), const, const],
        out_specs=row(D_MODEL, 0),
        out_shape=jax.ShapeDtypeStruct((m, D_MODEL), BF16),
        compiler_params=_cparams(("parallel",)),
        name="merge",
    )(o_sb, o_mla, w_pa, w_pb, proj, proj, bnorm_a_g.reshape(1, -1), bnorm_b_g.reshape(1, -1))


def _resid_norm_kernel(x_ref, m_ref, w_ref, g_ref, x1_ref, h_ref):
    x1 = x_ref[...] + jnp.dot(m_ref[...], w_ref[...], preferred_element_type=F32)
    x1_ref[...] = x1
    r = lax.rsqrt(jnp.mean(x1 * x1, axis=-1, keepdims=True) + RMS_EPS)
    h_ref[...] = (x1 * r * g_ref[...]).astype(h_ref.dtype)


def _resid_norm(x, mix, w_o, norm2_g, tm=256):
    m = x.shape[0]
    row = pl.BlockSpec((tm, D_MODEL), lambda i: (i, 0))
    return pl.pallas_call(
        _resid_norm_kernel,
        grid=(m // tm,),
        in_specs=[row, row, pl.BlockSpec(w_o.shape, lambda i: (0, 0)), pl.BlockSpec((1, D_MODEL), lambda i: (0, 0))],
        out_specs=[row, row],
        out_shape=[jax.ShapeDtypeStruct((m, D_MODEL), F32), jax.ShapeDtypeStruct((m, D_MODEL), BF16)],
        compiler_params=_cparams(("parallel",)),
        name="resid_norm",
    )(x, mix, w_o, norm2_g.reshape(1, -1))


def _young_pairs():
    return [(a, b) for a in range(PEER_TOPK) for b in range(PEER_TOPK) if (a + 1) * (b + 1) <= PEER_TOPK]


def _top_rows(cur, out_ref, count):
    for r in range(count):
        mx = jnp.max(cur, axis=0, keepdims=True)
        out_ref[r:r + 1, :] = mx
        cur = jnp.where(cur == mx, -jnp.inf, cur)


def _route_kernel(h_ref, wq_ref, sk_ref, s1_ref, s2_ref, e1_ref, e2_ref, th_ref, t1_ref, t2_ref, tc_ref):
    qt = lax.dot_general(wq_ref[...], h_ref[...], (((1,), (1,)), ((), ())),
                         preferred_element_type=F32).astype(BF16)
    k = PEER_TOPK
    for h in range(PEER_HEADS):
        base = h * 2 * PEER_HALF
        s1 = jnp.dot(sk_ref[2 * h], qt[base:base + PEER_HALF], preferred_element_type=F32)
        s2 = jnp.dot(sk_ref[2 * h + 1], qt[base + PEER_HALF:base + 2 * PEER_HALF], preferred_element_type=F32)
        _top_rows(s1, t1_ref, k)
        _top_rows(s2, t2_ref, k)
        t1 = t1_ref[...]
        t2 = t2_ref[...]
        groups = [t1[0:1] + t2, t1[1:2] + t2[0:8]]
        groups += [t1[a:a + 1] + t2[0:8] for a in range(2, 8)]
        groups.append(t1[8:16] + t2[0:1])
        cand = jnp.concatenate(groups, axis=0)
        _top_rows(cand, tc_ref, k)
        tc = tc_ref[...]
        mx = tc[0:1]
        z = jnp.sum(jnp.exp(tc - mx), axis=0, keepdims=True)
        s1_ref[h] = s1
        s2_ref[h] = s2
        e1_ref[h] = jnp.exp(s1 - t1[0:1]) / z
        e2_ref[h] = jnp.exp(s2 - t2[0:1])
        th_ref[h:h + 1, :] = tc[k - 1:k]


def _route(h2, w_pq_t, sub_keys2, tm=256):
    m = h2.shape[0]
    tab = pl.BlockSpec((PEER_HEADS, N_KEYS, tm), lambda i: (0, 0, i))
    tab_shape = jax.ShapeDtypeStruct((PEER_HEADS, N_KEYS, m), F32)
    return pl.pallas_call(
        _route_kernel,
        grid=(m // tm,),
        in_specs=[pl.BlockSpec((tm, D_MODEL), lambda i: (i, 0)),
                  pl.BlockSpec(w_pq_t.shape, lambda i: (0, 0)),
                  pl.BlockSpec(sub_keys2.shape, lambda i: (0, 0, 0))],
        out_specs=[tab, tab, tab, tab, pl.BlockSpec((PEER_HEADS, tm), lambda i: (0, i))],
        out_shape=[tab_shape, tab_shape, tab_shape, tab_shape, jax.ShapeDtypeStruct((PEER_HEADS, m), F32)],
        scratch_shapes=[pltpu.VMEM((PEER_TOPK, tm), F32), pltpu.VMEM((PEER_TOPK, tm), F32),
                        pltpu.VMEM((PEER_TOPK, tm), F32)],
        compiler_params=_cparams(("parallel",)),
        name="peer_route",
    )(h2, w_pq_t, sub_keys2)


def _gelu(x):
    return 0.5 * x * (1.0 + lax.erf(x * np.float32(math.sqrt(0.5))))


def _peer_kernel(rows_per_tile, h_ref, u_ref, vt_ref, s1_ref, s2_ref, e1_ref, e2_ref, th_ref, o_ref, w_ref):
    j = pl.program_id(1)
    tm = h_ref.shape[0]

    @pl.when(j == 0)
    def _():
        o_ref[...] = jnp.zeros_like(o_ref)

    for r in range(rows_per_tile):
        for c in range(tm // LANE):
            cols = slice(c * LANE, (c + 1) * LANE)
            acc = jnp.zeros((N_KEYS, LANE), F32)
            for h in range(PEER_HEADS):
                s1row = s1_ref[h, r:r + 1, cols]
                e1row = e1_ref[h, r:r + 1, cols]
                hit = (s1row + s2_ref[h, :, cols]) >= th_ref[h:h + 1, cols]
                acc = acc + jnp.where(hit, e1row * e2_ref[h, :, cols], 0.0)
            w_ref[r * N_KEYS:(r + 1) * N_KEYS, cols] = acc

    act = lax.dot_general(u_ref[...], h_ref[...], (((1,), (1,)), ((), ())), preferred_element_type=F32)
    g = (_gelu(act) * w_ref[...]).astype(BF16)
    o_ref[...] += jnp.dot(vt_ref[...], g, preferred_element_type=F32)


def _peer(h2, u, vt, s1, s2, e1, e2, th, tm=512, te=1024):
    m = h2.shape[0]
    n_exp = u.shape[0]
    rows = te // N_KEYS
    tab = pl.BlockSpec((PEER_HEADS, N_KEYS, tm), lambda i, j: (0, 0, i))
    tab_rows = pl.BlockSpec((PEER_HEADS, rows, tm), lambda i, j: (0, j, i))
    return pl.pallas_call(
        functools.partial(_peer_kernel, rows),
        grid=(m // tm, n_exp // te),
        in_specs=[pl.BlockSpec((tm, D_MODEL), lambda i, j: (i, 0)),
                  pl.BlockSpec((te, D_MODEL), lambda i, j: (j, 0)),
                  pl.BlockSpec((D_MODEL, te), lambda i, j: (0, j)),
                  tab_rows, tab, tab_rows, tab,
                  pl.BlockSpec((PEER_HEADS, tm), lambda i, j: (0, i))],
        out_specs=pl.BlockSpec((D_MODEL, tm), lambda i, j: (0, i)),
        out_shape=jax.ShapeDtypeStruct((D_MODEL, m), F32),
        scratch_shapes=[pltpu.VMEM((te, tm), F32)],
        compiler_params=_cparams(("parallel", "arbitrary")),
        name="peer_experts",
    )(h2, u, vt, s1, s2, e1, e2, th)


def _rope_tables(pos):
    half = ROPE_DIM // 2
    inv = jnp.power(jnp.float32(ROPE_THETA), -jnp.arange(half, dtype=F32) / half)
    ang = pos.astype(F32)[:, None] * inv[None, :]
    cos, sin = jnp.cos(ang), jnp.sin(ang)
    return jnp.tile(jnp.concatenate([cos, cos], axis=1), (1, 2)), jnp.tile(jnp.concatenate([-sin, sin], axis=1), (1, 2))


def _round_up(x, m):
    return (x + m - 1) // m * m


def kernel(x_prompt, x_sample, cache_sb_kv, cache_mla, page_table, meta_tokens, norm1_g, w_in, q_nope_g, q_rope_g, ckv_g, krope_g, w_uk, w_uv, w_pa, w_pb, bnorm_a_g, bnorm_b_g, w_o, norm2_g, w_pq, sub_keys, expert_u, expert_v):
    batch, seq, d = x_prompt.shape
    db, ds, _ = x_sample.shape
    depth, n_pool, page = cache_sb_kv.shape[:3]
    assert depth == 1 and ds == 1 and d == D_MODEL and page == SEQ_TILE
    t_real = N_META + seq
    tp = _round_up(t_real, SEQ_TILE)
    rows_p = batch * tp
    assert rows_p % db == 0 and db % 8 == 0
    m = _round_up(rows_p + db, ROW_TILE)
    past_len = page_table.shape[1] * page

    meta = jnp.broadcast_to(meta_tokens[None].astype(F32), (batch, N_META, d))
    xp = jnp.concatenate([meta, x_prompt, jnp.zeros((batch, tp - t_real, d), F32)], axis=1).reshape(rows_p, d)
    x_all = jnp.concatenate([xp, x_sample.reshape(db, d), jnp.zeros((m - rows_p - db, d), F32)], axis=0)
    pos = jnp.concatenate([jnp.tile(jnp.arange(tp), batch), jnp.full((m - rows_p,), past_len)])
    cos_t, sin_t = _rope_tables(pos)

    w = w_in[0]
    q_mla_w = w[:, 1536:3072].reshape(d, MLA_HEADS, NOPE_DIM + ROPE_DIM)
    w_perm = jnp.concatenate([
        w[:, 0:1024], q_mla_w[:, :, :NOPE_DIM].reshape(d, -1), w[:, 3648:5696], w[:, 5696:7744],
        w[:, 1024:1280], w[:, 1280:1536], q_mla_w[:, :, NOPE_DIM:].reshape(d, -1), w[:, 3072:3584],
        w[:, 3584:3648], jnp.zeros((d, PROJ_COLS - 7744), F32)], axis=1).astype(BF16)
    w_uk2 = w_uk[0].reshape(KV_LORA, MLA_HEADS * NOPE_DIM).astype(BF16)
    w_uv2 = w_uv[0].reshape(KV_LORA, MLA_HEADS * MLA_V_DIM).astype(BF16)
    w_ukv = jnp.concatenate([w_uk2, w_uv2], axis=1)

    h1 = _rmsnorm(x_all, norm1_g[0])
    proj = _matmul(h1, w_perm, 512, 512, F32, "in_proj")
    qcat, mla = _postproj(proj, cos_t, sin_t, q_nope_g[0], q_rope_g[0], ckv_g[0], krope_g[0])

    kcat, v_mla = _kvup(mla, w_ukv, rows_p)
    o_sb_p = _sb_prompt(proj, batch, tp)
    o_mla_p = _mla_prompt(qcat, kcat, v_mla, batch, tp)

    srows = slice(rows_p, rows_p + db)
    qml = jnp.transpose(_qlat(qcat, w_uk2, rows_p, db), (1, 0, 2))
    qsb = proj[srows, C_QSB:C_QSB + SB_WIDTH].reshape(db, SB_HEADS, HEAD_DIM)
    knew = proj[srows, C_KSB:C_KSB + 4 * HEAD_DIM].reshape(db, 1, 4 * HEAD_DIM)
    lnew = mla[srows].reshape(db, 1, MLA_CACHE_W)
    o_sb_s, acc_s = _decode(qsb, qml, knew, lnew,
                            cache_sb_kv[0].reshape(n_pool, page, 4 * HEAD_DIM), cache_mla[0], page_table)
    o_mla_s = _uvout(jnp.transpose(acc_s, (1, 0, 2)), w_uv2)

    pad = jnp.zeros((m - rows_p - db, SB_WIDTH), BF16)
    o_sb = jnp.concatenate([o_sb_p, o_sb_s.reshape(db, SB_WIDTH), pad], axis=0)
    o_mla = jnp.concatenate([o_mla_p, o_mla_s, pad], axis=0)
    mix = _merge(o_sb, o_mla, w_pa[0].astype(BF16), w_pb[0].astype(BF16), proj, bnorm_a_g[0], bnorm_b_g[0])
    x1, h2 = _resid_norm(x_all, mix, w_o[0].astype(BF16), norm2_g[0])
    sk2 = sub_keys[0].reshape(PEER_HEADS * 2, N_KEYS, PEER_HALF).astype(BF16)
    s1, s2, e1, e2, th = _route(h2, w_pq[0].T.astype(BF16), sk2)
    peer_t = _peer(h2, expert_u[0].astype(BF16), expert_v[0].T.astype(BF16), s1, s2, e1, e2, th)
    x2 = x1 + peer_t.T

    y_prompt = x2[:rows_p].reshape(batch, tp, d)[:, N_META:t_real]
    y_sample = x2[srows].reshape(db, 1, d)
    kv_cols = proj[:, C_KSB:C_KSB + 4 * HEAD_DIM]
    sb_kv_prompt = kv_cols[:rows_p].reshape(batch, tp, 2, SB_KV_HEADS, HEAD_DIM)[:, :t_real][None]
    mla_prompt = mla[:rows_p].reshape(batch, tp, MLA_CACHE_W)[:, :t_real][None]
    sb_kv_sample = kv_cols[srows].reshape(1, db, 1, 2, SB_KV_HEADS, HEAD_DIM)
    mla_sample = mla[srows].reshape(1, db, 1, MLA_CACHE_W)
    return y_prompt, y_sample, sb_kv_prompt, mla_prompt, sb_kv_sample, mla_sample
```
